```python
import math
import jax, jax.numpy as jnp
from jax import lax
import numpy as np

D_MODEL = 1024
BATCH = 32
SEQ = 2048
DEPTH = 1

A_WIDTH = D_MODEL
A_HEADS = 8
A_HEAD_DIM = A_WIDTH // A_HEADS
CHUNK = 128
B_WIDTH = D_MODEL // 2
B_GROUP = 16
B_GROUPS = B_WIDTH // B_GROUP
STATE = 64
DT_MIN = 1e-3
DT_MAX = 1e-1
EPS = 1e-6
IN_COLS = 3 * A_WIDTH + 2 * B_WIDTH + 2 * D_MODEL
SPLITS = (A_WIDTH, 2 * A_WIDTH, 3 * A_WIDTH, 3 * A_WIDTH + B_WIDTH,
          3 * A_WIDTH + 2 * B_WIDTH, 3 * A_WIDTH + 2 * B_WIDTH + D_MODEL)

kernel_name = "hybrid_gmlp_s5_gated_block"


def rmsnorm(x, g):
    xf = x.astype(jnp.float32)
    y = xf * lax.rsqrt(jnp.mean(xf * xf, axis=-1, keepdims=True) + EPS)
    return (y * g.astype(jnp.float32)).astype(x.dtype)


def layernorm(x, g, b):
    xf = x.astype(jnp.float32)
    mu = jnp.mean(xf, axis=-1, keepdims=True)
    xc = xf - mu
    y = xc * lax.rsqrt(jnp.mean(xc * xc, axis=-1, keepdims=True) + EPS)
    return (y * g.astype(jnp.float32) + b.astype(jnp.float32)).astype(x.dtype)


def gmlp_spatial_gating(u, v, ln_g, ln_b, w_s, b_s):
    bsz, seq, _ = v.shape
    v = layernorm(v, ln_g, ln_b)
    vc = v.reshape(bsz, seq // CHUNK, CHUNK, A_HEADS, A_HEAD_DIM)
    causal = jnp.tril(jnp.ones((CHUNK, CHUNK), dtype=bool))
    w = jnp.where(causal[None], w_s, 0)
    mixed = jnp.einsum('hts,bcshd->bcthd', w, vc) + jnp.transpose(b_s)[:, :, None]
    return u * mixed.reshape(bsz, seq, A_WIDTH)


def s5_scan(xb, lam_re, lam_im, log_dt, b_re, b_im, c_re, c_im, d_skip):
    bsz, seq, _ = xb.shape
    f32 = jnp.float32
    u = xb.astype(f32).reshape(bsz, seq, B_GROUPS, B_GROUP)
    dt = jnp.exp(log_dt.astype(f32))[:, None]
    lr = lam_re.astype(f32)
    li = lam_im.astype(f32)
    mag = jnp.exp(lr * dt)
    ab_re = mag * jnp.cos(li * dt)
    ab_im = mag * jnp.sin(li * dt)
    den = lr * lr + li * li
    nr = ab_re - 1.0
    ni = ab_im
    k_re = ((nr * lr + ni * li) / den)[..., None]
    k_im = ((ni * lr - nr * li) / den)[..., None]
    br = b_re.astype(f32)
    bi = b_im.astype(f32)
    bb_re = k_re * br - k_im * bi
    bb_im = k_re * bi + k_im * br
    bu_re = jnp.einsum('gph,bsgh->bsgp', bb_re, u)
    bu_im = jnp.einsum('gph,bsgh->bsgp', bb_im, u)
    a_re = jnp.broadcast_to(ab_re[None, None], (1, seq, B_GROUPS, STATE))
    a_im = jnp.broadcast_to(ab_im[None, None], (1, seq, B_GROUPS, STATE))

    def combine(e1, e2):
        ar1, ai1, br1, bi1 = e1
        ar2, ai2, br2, bi2 = e2
        return (ar1 * ar2 - ai1 * ai2,
                ar1 * ai2 + ai1 * ar2,
                ar2 * br1 - ai2 * bi1 + br2,
                ar2 * bi1 + ai2 * br1 + bi2)

    _, _, h_re, h_im = lax.associative_scan(combine, (a_re, a_im, bu_re, bu_im), axis=1)
    y = (jnp.einsum('ghp,bsgp->bsgh', c_re.astype(f32), h_re)
         - jnp.einsum('ghp,bsgp->bsgh', c_im.astype(f32), h_im))
    y = y.reshape(bsz, seq, B_WIDTH) + d_skip.astype(f32) * xb.astype(f32)
    return y.astype(xb.dtype)


def setup_inputs(seed: int = 0) -> dict:
    key = jax.random.key(seed)
    ks = jax.random.split(key, 24)
    f32 = jnp.float32
    nrm = lambda k, shape, scale: (jax.random.normal(k, shape, f32) * scale)
    n_idx = jnp.arange(STATE, dtype=f32)
    lam_re = -0.5 + nrm(ks[7], (DEPTH, B_GROUPS, STATE), 0.01)
    lam_im = math.pi * n_idx[None, None, :] + nrm(ks[8], (DEPTH, B_GROUPS, STATE), 0.01)
    log_dt = jax.random.uniform(ks[9], (DEPTH, B_GROUPS), f32,
                                math.log(DT_MIN), math.log(DT_MAX))
    b_scale = (B_GROUP ** -0.5) / math.sqrt(2.0)
    c_scale = (STATE ** -0.5)
    return {
        "x": nrm(ks[0], (BATCH, SEQ, D_MODEL), 1.0),
        "norm_gain": 1.0 + nrm(ks[1], (DEPTH, D_MODEL), 0.01),
        "w_in": nrm(ks[2], (DEPTH, D_MODEL, IN_COLS), D_MODEL ** -0.5),
        "a_ln_gain": 1.0 + nrm(ks[3], (DEPTH, A_WIDTH), 0.01),
        "a_ln_bias": nrm(ks[4], (DEPTH, A_WIDTH), 0.01),
        "a_spatial": nrm(ks[5], (DEPTH, A_HEADS, CHUNK, CHUNK), CHUNK ** -0.5),
        "a_spatial_bias": 1.0 + nrm(ks[6], (DEPTH, A_HEADS, CHUNK), 0.1),
        "w_a_down": nrm(ks[10], (DEPTH, A_WIDTH, D_MODEL), A_WIDTH ** -0.5),
        "lambda_re": lam_re,
        "lambda_im": lam_im,
        "log_dt": log_dt,
        "b_re": nrm(ks[11], (DEPTH, B_GROUPS, STATE, B_GROUP), b_scale),
        "b_im": nrm(ks[12], (DEPTH, B_GROUPS, STATE, B_GROUP), b_scale),
        "c_re": nrm(ks[13], (DEPTH, B_GROUPS, B_GROUP, STATE), c_scale),
        "c_im": nrm(ks[14], (DEPTH, B_GROUPS, B_GROUP, STATE), c_scale),
        "d_skip": nrm(ks[15], (DEPTH, B_WIDTH), 1.0),
        "w_glu": nrm(ks[16], (DEPTH, B_WIDTH, 2 * B_WIDTH), B_WIDTH ** -0.5),
        "w_b_down": nrm(ks[17], (DEPTH, B_WIDTH, D_MODEL), B_WIDTH ** -0.5),
        "w_out": nrm(ks[18], (DEPTH, D_MODEL, D_MODEL), D_MODEL ** -0.5),
        "final_gain": 1.0 + nrm(ks[19], (D_MODEL,), 0.01),
    }


def reference(x, norm_gain, w_in, a_ln_gain, a_ln_bias, a_spatial, a_spatial_bias, w_a_down,
              lambda_re, lambda_im, log_dt, b_re, b_im, c_re, c_im, d_skip, w_glu, w_b_down,
              w_out, final_gain):
    h = x
    for l in range(DEPTH):
        xn = rmsnorm(h, norm_gain[l])
        proj = jnp.einsum('bsd,dc->bsc', xn, w_in[l])
        u_a, v_a, z_a, x_b, z_b, g_a, g_b = jnp.split(proj, SPLITS, axis=-1)
        y_a = gmlp_spatial_gating(jax.nn.gelu(u_a), jax.nn.gelu(v_a), a_ln_gain[l], a_ln_bias[l],
                                  a_spatial[l], a_spatial_bias[l])
        y_a = jnp.einsum('bsc,cd->bsd', y_a * jax.nn.silu(z_a), w_a_down[l])
        y_b = s5_scan(x_b, lambda_re[l], lambda_im[l], log_dt[l], b_re[l], b_im[l],
                      c_re[l], c_im[l], d_skip[l])
        glu_p, glu_q = jnp.split(jnp.einsum('bsc,ce->bse', jax.nn.gelu(y_b), w_glu[l]), 2, axis=-1)
        y_b = glu_p * jax.nn.sigmoid(glu_q)
        y_b = jnp.einsum('bsc,cd->bsd', y_b * jax.nn.silu(z_b), w_b_down[l])
        merged = jax.nn.sigmoid(g_a) * y_a + jax.nn.sigmoid(g_b) * y_b
        h = h + jnp.einsum('bsd,de->bse', merged, w_out[l])
    return rmsnorm(h, final_gain)
```

```python
import functools
import math

import jax
import jax.numpy as jnp
from jax import lax
from jax.experimental import pallas as pl
from jax.experimental.pallas import tpu as pltpu

F32 = jnp.float32
BF16 = jnp.bfloat16

D_MODEL = 1024
A_WIDTH = D_MODEL
A_HEADS = 8
A_HEAD_DIM = A_WIDTH // A_HEADS
CHUNK = 128
B_WIDTH = D_MODEL // 2
B_GROUP = 16
B_GROUPS = B_WIDTH // B_GROUP
STATE = 64
EPS = 1e-6

S5_T = 16
S5_FLAT = S5_T * B_GROUP
S5_PAIRS = B_GROUPS // 2

C_U, C_V, C_Z, C_ZB, C_GA, C_GB = 0, 1024, 2048, 3072, 3584, 4608
W1_COLS = 5632

V7X_VMEM_LIMIT_BYTES = 60 * 1024 * 1024

_HI = lax.Precision.HIGHEST


def _gelu(x):
    c = math.sqrt(2.0 / math.pi)
    inner = x * (c + (c * 0.044715) * (x * x))
    hx = 0.5 * x
    return hx + hx * jnp.tanh(inner)


def _sigmoid(x):
    return 0.5 * jnp.tanh(0.5 * x) + 0.5


def _silu(x):
    hx = 0.5 * x
    return hx + hx * jnp.tanh(hx)


def _rmsnorm(x, g):
    ms = jnp.mean(x * x, axis=-1, keepdims=True)
    return (x * lax.rsqrt(ms + EPS)) * g


def _dot(a, b):
    return jnp.dot(a, b, preferred_element_type=F32)


def _xb_kernel(x_ref, g_ref, w_ref, o_ref):
    xn = _rmsnorm(x_ref[...], g_ref[...]).astype(BF16)
    o_ref[...] = _dot(xn, w_ref[...]).astype(BF16)


def _xb_call(x2, norm_gain, w_xb, tm):
    n = x2.shape[0]
    return pl.pallas_call(
        _xb_kernel,
        grid=(n // tm,),
        in_specs=[
            pl.BlockSpec((tm, D_MODEL), lambda i: (i, 0)),
            pl.BlockSpec((1, D_MODEL), lambda i: (0, 0)),
            pl.BlockSpec((D_MODEL, B_WIDTH), lambda i: (0, 0)),
        ],
        out_specs=pl.BlockSpec((tm, B_WIDTH), lambda i: (i, 0)),
        out_shape=jax.ShapeDtypeStruct((n, B_WIDTH), BF16),
        compiler_params=pltpu.CompilerParams(
            dimension_semantics=("arbitrary",), vmem_limit_bytes=V7X_VMEM_LIMIT_BYTES),
        name="s5_in_proj",
    )(x2, norm_gain, w_xb)


def _s5_kernel(z_ref, m_ref, wst_ref, wout_ref, a_ref, o_ref, s_scr, hp_scr, *, n_rows, bsz):
    z0 = z_ref[0, 0]
    z1 = z_ref[0, 1]
    s_scr[...] = _dot(jnp.concatenate([z0, z1], axis=1), wst_ref[0])
    a_re = jnp.broadcast_to(a_ref[0, 0:1, :], (bsz, 2 * STATE))
    a_im = jnp.broadcast_to(a_ref[0, 1:2, :], (bsz, 2 * STATE))

    def step(r, carry):
        h_re, h_im = carry
        rows = pl.ds(pl.multiple_of(r * bsz, bsz), bsz)
        hp_scr[rows, 0:2 * STATE] = h_re.astype(BF16)
        hp_scr[rows, 2 * STATE:4 * STATE] = h_im.astype(BF16)
        s = s_scr[rows, :]
        n_re = a_re * h_re - a_im * h_im + s[:, 0:2 * STATE]
        n_im = a_re * h_im + a_im * h_re + s[:, 2 * STATE:4 * STATE]
        return n_re, n_im

    zero = jnp.zeros((bsz, 2 * STATE), F32)
    lax.fori_loop(0, n_rows, step, (zero, zero))
    hp = hp_scr[...]
    wout = wout_ref[0]
    o_ref[0, 0] = (_dot(z0, m_ref[0, 0]) + _dot(hp, wout[:, 0:S5_FLAT])).astype(BF16)
    o_ref[0, 1] = (_dot(z1, m_ref[0, 1]) + _dot(hp, wout[:, S5_FLAT:2 * S5_FLAT])).astype(BF16)


def _s5_call(zf, m, wst, wout, a16, bsz):
    rows = zf.shape[2]
    n_rows = rows // bsz
    kern = functools.partial(_s5_kernel, n_rows=n_rows, bsz=bsz)
    return pl.pallas_call(
        kern,
        grid=(S5_PAIRS,),
        in_specs=[
            pl.BlockSpec((1, 2, rows, S5_FLAT), lambda j: (j, 0, 0, 0)),
            pl.BlockSpec((1, 2, S5_FLAT, S5_FLAT), lambda j: (j, 0, 0, 0)),
            pl.BlockSpec((1, 2 * S5_FLAT, 4 * STATE), lambda j: (j, 0, 0)),
            pl.BlockSpec((1, 4 * STATE, 2 * S5_FLAT), lambda j: (j, 0, 0)),
            pl.BlockSpec((1, 2, 2 * STATE), lambda j: (j, 0, 0)),
        ],
        out_specs=pl.BlockSpec((1, 2, rows, S5_FLAT), lambda j: (j, 0, 0, 0)),
        out_shape=jax.ShapeDtypeStruct(zf.shape, BF16),
        scratch_shapes=[
            pltpu.VMEM((rows, 4 * STATE), F32),
            pltpu.VMEM((rows, 4 * STATE), BF16),
        ],
        compiler_params=pltpu.CompilerParams(
            dimension_semantics=("arbitrary",), vmem_limit_bytes=V7X_VMEM_LIMIT_BYTES),
        name="s5_scan",
    )(zf, m, wst, wout, a16)


def _s5_matrices(lambda_re, lambda_im, log_dt, b_re, b_im, c_re, c_im, d_skip):
    g, p = B_GROUPS, STATE
    dt = jnp.exp(log_dt.astype(F32))[:, None]
    lr = lambda_re.astype(F32)
    li = lambda_im.astype(F32)
    mag = jnp.exp(lr * dt)
    ab_re = mag * jnp.cos(li * dt)
    ab_im = mag * jnp.sin(li * dt)
    den = lr * lr + li * li
    nr = ab_re - 1.0
    ni = ab_im
    k_re = ((nr * lr + ni * li) / den)[..., None]
    k_im = ((ni * lr - nr * li) / den)[..., None]
    br = b_re.astype(F32)
    bi = b_im.astype(F32)
    bb_re = k_re * br - k_im * bi
    bb_im = k_re * bi + k_im * br
    lag = jnp.arange(S5_T + 1, dtype=F32)[:, None, None]
    pmag = jnp.exp(lag * (lr * dt)[None])
    pw_re = pmag * jnp.cos(lag * (li * dt)[None])
    pw_im = pmag * jnp.sin(lag * (li * dt)[None])
    cr = c_re.astype(F32)
    ci = c_im.astype(F32)
    e_re = pw_re[..., None] * bb_re[None] - pw_im[..., None] * bb_im[None]
    e_im = pw_re[..., None] * bb_im[None] + pw_im[..., None] * bb_re[None]
    kern = (jnp.einsum('gop,lgpi->lgoi', cr, e_re, precision=_HI)
            - jnp.einsum('gop,lgpi->lgoi', ci, e_im, precision=_HI))
    s_idx = jnp.arange(S5_T)[:, None]
    t_idx = jnp.arange(S5_T)[None, :]
    lagm = jnp.clip(t_idx - s_idx, 0, S5_T)
    kt = kern[lagm]
    kt = jnp.where((t_idx >= s_idx)[:, :, None, None, None], kt, 0.0)
    m = jnp.transpose(kt, (2, 0, 4, 1, 3))
    dsk = d_skip.astype(F32).reshape(g, B_GROUP)
    eye_t = jnp.eye(S5_T, dtype=F32)
    eye_h = jnp.eye(B_GROUP, dtype=F32)
    m = m + (dsk[:, None, :, None, None] * eye_t[None, :, None, :, None]
             * eye_h[None, None, :, None, :])
    m = m.reshape(S5_PAIRS, 2, S5_FLAT, S5_FLAT)
    wst_re = jnp.transpose(e_re[S5_T - 1::-1], (1, 0, 3, 2)).reshape(g, S5_FLAT, p)
    wst_im = jnp.transpose(e_im[S5_T - 1::-1], (1, 0, 3, 2)).reshape(g, S5_FLAT, p)
    f_re = cr[None] * pw_re[:, :, None, :] - ci[None] * pw_im[:, :, None, :]
    f_im = cr[None] * pw_im[:, :, None, :] + ci[None] * pw_re[:, :, None, :]
    wo_re = jnp.transpose(f_re[1:], (1, 3, 0, 2)).reshape(g, p, S5_FLAT)
    wo_im = -jnp.transpose(f_im[1:], (1, 3, 0, 2)).reshape(g, p, S5_FLAT)
    eye2 = jnp.eye(2, dtype=F32)

    def pair_rows(w):
        k, n = w.shape[1], w.shape[2]
        w = w.reshape(S5_PAIRS, 2, k, n)
        return (w[:, :, :, None, :] * eye2[None, :, None, :, None]).reshape(S5_PAIRS, 2 * k, 2 * n)

    wst = jnp.concatenate([pair_rows(wst_re), pair_rows(wst_im)], axis=2)
    wout = jnp.concatenate([pair_rows(wo_re), pair_rows(wo_im)], axis=1)
    a16 = jnp.stack([pw_re[S5_T].reshape(S5_PAIRS, 2 * p),
                     pw_im[S5_T].reshape(S5_PAIRS, 2 * p)], axis=1)
    return m.astype(BF16), wst.astype(BF16), wout.astype(BF16), a16


def _main_kernel(x_ref, ys_ref, ng_ref, w1_ref, lng_ref, lnb_ref, ws_ref, sb_ref, wad_ref,
                 wglu_ref, wbd_ref, wout_ref, fg_ref, o_ref, vn_s, mix_s, *, tm):
    nch = tm // CHUNK
    x = x_ref[...]
    xn = _rmsnorm(x, ng_ref[...]).astype(BF16)

    def proj(c0, c1):
        return _dot(xn, w1_ref[:, c0:c1])

    v = _gelu(proj(C_V, C_V + A_WIDTH))
    mu = jnp.mean(v, axis=-1, keepdims=True)
    vc = v - mu
    var = jnp.mean(vc * vc, axis=-1, keepdims=True)
    vn_s[...] = ((vc * lax.rsqrt(var + EPS)) * lng_ref[...] + lnb_ref[...]).astype(BF16)
    for h in range(A_HEADS):
        cols = slice(h * A_HEAD_DIM, (h + 1) * A_HEAD_DIM)
        vh = jnp.concatenate(
            [vn_s[c * CHUNK:(c + 1) * CHUNK, cols] for c in range(nch)], axis=1)
        mh = _dot(ws_ref[h], vh)
        for c in range(nch):
            mix_s[c * CHUNK:(c + 1) * CHUNK, cols] = (
                mh[:, c * A_HEAD_DIM:(c + 1) * A_HEAD_DIM] + sb_ref[:, cols])
    u = _gelu(proj(C_U, C_U + A_WIDTH))
    za = _silu(proj(C_Z, C_Z + A_WIDTH))
    ya = _dot((u * mix_s[...] * za).astype(BF16), wad_ref[...])
    merged = _sigmoid(proj(C_GA, C_GA + D_MODEL)) * ya
    t = _gelu(ys_ref[...].astype(F32)).astype(BF16)
    pq = _dot(t, wglu_ref[...])
    zb = _silu(proj(C_ZB, C_ZB + B_WIDTH))
    yb_pre = pq[:, 0:B_WIDTH] * _sigmoid(pq[:, B_WIDTH:2 * B_WIDTH]) * zb
    yb = _dot(yb_pre.astype(BF16), wbd_ref[...])
    merged = merged + _sigmoid(proj(C_GB, C_GB + D_MODEL)) * yb
    hres = x + _dot(merged.astype(BF16), wout_ref[...])
    o_ref[...] = _rmsnorm(hres, fg_ref[...])


def _main_call(x2, ys, norm_gain, w1, ln_g, ln_b, ws, sb, wad, wglu, wbd, wout, fgain, tm):
    n = x2.shape[0]

    def const(shape):
        nd = len(shape)
        return pl.BlockSpec(shape, lambda i: (0,) * nd, pipeline_mode=pl.Buffered(1))

    kern = functools.partial(_main_kernel, tm=tm)
    return pl.pallas_call(
        kern,
        grid=(n // tm,),
        in_specs=[
            pl.BlockSpec((tm, D_MODEL), lambda i: (i, 0)),
            pl.BlockSpec((tm, B_WIDTH), lambda i: (i, 0)),
            const((1, D_MODEL)),
            const((D_MODEL, W1_COLS)),
            const((1, A_WIDTH)),
            const((1, A_WIDTH)),
            const((A_HEADS, CHUNK, CHUNK)),
            const((CHUNK, A_WIDTH)),
            const((A_WIDTH, D_MODEL)),
            const((B_WIDTH, 2 * B_WIDTH)),
            const((B_WIDTH, D_MODEL)),
            const((D_MODEL, D_MODEL)),
            const((1, D_MODEL)),
        ],
        out_specs=pl.BlockSpec((tm, D_MODEL), lambda i: (i, 0)),
        out_shape=jax.ShapeDtypeStruct((n, D_MODEL), F32),
        scratch_shapes=[
            pltpu.VMEM((tm, A_WIDTH), BF16),
            pltpu.VMEM((tm, A_WIDTH), F32),
        ],
        compiler_params=pltpu.CompilerParams(
            dimension_semantics=("arbitrary",), vmem_limit_bytes=V7X_VMEM_LIMIT_BYTES),
        name="fused_block",
    )(x2, ys, norm_gain, w1, ln_g, ln_b, ws, sb, wad, wglu, wbd, wout, fgain)


def kernel(x, norm_gain, w_in, a_ln_gain, a_ln_bias, a_spatial, a_spatial_bias, w_a_down,
           lambda_re, lambda_im, log_dt, b_re, b_im, c_re, c_im, d_skip, w_glu, w_b_down,
           w_out, final_gain):
    assert norm_gain.shape[0] == 1, "single-layer block"
    bsz, seq, d = x.shape
    assert d == D_MODEL and seq % CHUNK == 0 and bsz % 16 == 0
    n = bsz * seq
    n_rows = seq // S5_T
    x2 = x.reshape(n, D_MODEL)
    w_in0 = w_in[0]
    ngain = norm_gain[0].reshape(1, D_MODEL).astype(F32)

    w_xb = w_in0[:, 3 * A_WIDTH:3 * A_WIDTH + B_WIDTH].astype(BF16)
    xb = _xb_call(x2, ngain, w_xb, tm=min(1024, n))

    zf = xb.reshape(bsz, n_rows, S5_T, B_GROUPS, B_GROUP)
    zf = jnp.transpose(zf, (3, 1, 0, 2, 4)).reshape(S5_PAIRS, 2, n_rows * bsz, S5_FLAT)
    m, wst, wo, a16 = _s5_matrices(lambda_re[0], lambda_im[0], log_dt[0], b_re[0], b_im[0],
                                   c_re[0], c_im[0], d_skip[0])
    yf = _s5_call(zf, m, wst, wo, a16, bsz)
    ys = yf.reshape(B_GROUPS, n_rows, bsz, S5_T, B_GROUP)
    ys = jnp.transpose(ys, (2, 1, 3, 0, 4)).reshape(n, B_WIDTH)

    w1 = jnp.concatenate(
        [w_in0[:, 0:3 * A_WIDTH], w_in0[:, 3 * A_WIDTH + B_WIDTH:]], axis=1).astype(BF16)
    causal = jnp.tril(jnp.ones((CHUNK, CHUNK), dtype=bool))
    ws = jnp.where(causal[None], a_spatial[0], 0).astype(BF16)
    sb = jnp.repeat(jnp.transpose(a_spatial_bias[0]).astype(F32), A_HEAD_DIM, axis=1)
    out = _main_call(
        x2, ys, ngain, w1,
        a_ln_gain[0].reshape(1, A_WIDTH).astype(F32), a_ln_bias[0].reshape(1, A_WIDTH).astype(F32),
        ws, sb, w_a_down[0].astype(BF16), w_glu[0].astype(BF16), w_b_down[0].astype(BF16),
        w_out[0].astype(BF16), final_gain.reshape(1, D_MODEL).astype(F32), tm=256)
    return out.reshape(bsz, seq, D_MODEL)
```

```python
import functools
import math

import jax
import jax.numpy as jnp
from jax import lax
from jax.experimental import pallas as pl
from jax.experimental.pallas import tpu as pltpu

F32 = jnp.float32
BF16 = jnp.bfloat16

D_MODEL = 1024
A_WIDTH = D_MODEL
A_HEADS = 8
A_HEAD_DIM = A_WIDTH // A_HEADS
CHUNK = 128
B_WIDTH = D_MODEL // 2
B_GROUP = 16
B_GROUPS = B_WIDTH // B_GROUP
STATE = 64
EPS = 1e-6

LANES = 128
SUBLANES = 8
LANE_TILES_B = B_WIDTH // LANES
GROUPS_PER_TILE = LANES // B_GROUP

S5_T = 16
S5_FLAT = S5_T * B_GROUP
S5_PAIRS = B_GROUPS // 2
ROWS_PER_CHUNK = CHUNK // S5_T

XB_NB = 16
MAIN_NB = 2

C_U, C_V, C_Z, C_ZB, C_GA, C_GB = 0, 1024, 2048, 3072, 3584, 4608
W1_COLS = 5632

V7X_VMEM_LIMIT_BYTES = 60 * 1024 * 1024

_HI = lax.Precision.HIGHEST


def _gelu(x):
    c = math.sqrt(2.0 / math.pi)
    inner = x * (c + (c * 0.044715) * (x * x))
    hx = 0.5 * x
    return hx + hx * jnp.tanh(inner)


def _sigmoid(x):
    return 0.5 * jnp.tanh(0.5 * x) + 0.5


def _silu(x):
    hx = 0.5 * x
    return hx + hx * jnp.tanh(hx)


def _rmsnorm(x, g):
    ms = jnp.mean(x * x, axis=-1, keepdims=True)
    return (x * lax.rsqrt(ms + EPS)) * g


def _dot(a, b):
    return jnp.dot(a, b, preferred_element_type=F32)


def _placement_matrices():
    r = jnp.arange(GROUPS_PER_TILE * LANES)
    c = jnp.arange(2 * LANES)
    r_hi, r_mid, r_lo = r // LANES, (r % LANES) // B_GROUP, r % B_GROUP
    c_hi, c_mid, c_lo = c // LANES, (c % LANES) // B_GROUP, c % B_GROUP
    pair = jnp.arange(4)[:, None, None]
    q = ((r_lo[None, :, None] == c_lo[None, None, :])
         & (r_hi[None, :, None] == c_mid[None, None, :])
         & (r_mid[None, :, None] == 2 * pair + c_hi[None, None, :]))
    return q.astype(BF16)


def _xb_kernel(x_ref, g_ref, w_ref, q_ref, o_ref, xb_s):
    nb, npos, _ = x_ref.shape
    nrow = nb * npos // S5_T
    x = x_ref[...].reshape(nb * npos, D_MODEL)
    xn = _rmsnorm(x, g_ref[...]).astype(BF16)
    xb = _dot(xn, w_ref[...])
    for j in range(LANE_TILES_B):
        xb_s[j] = xb[:, j * LANES:(j + 1) * LANES]
    blocks = []
    for j in range(LANE_TILES_B):
        for half in range(2):
            blocks.append(jnp.concatenate(
                [xb_s[j, pl.ds(half * SUBLANES + tl, nrow, stride=S5_T), :]
                 for tl in range(SUBLANES)], axis=1).astype(BF16))
    lhs = jnp.concatenate(blocks, axis=0)
    for glp in range(4):
        res = _dot(lhs, q_ref[glp]).astype(BF16)
        for j in range(LANE_TILES_B):
            for half in range(2):
                base = (j * 2 + half) * nrow
                for glo in range(2):
                    g = GROUPS_PER_TILE * j + 2 * glp + glo
                    o_ref[g, :, half * LANES:(half + 1) * LANES] = (
                        res[base:base + nrow, glo * LANES:(glo + 1) * LANES])


def _xb_call(x, norm_gain, w_xb, q):
    bsz, seq, _ = x.shape
    n_c, n_bh = seq // CHUNK, bsz // XB_NB
    nrow = XB_NB * ROWS_PER_CHUNK
    return pl.pallas_call(
        _xb_kernel,
        grid=(n_c, n_bh),
        in_specs=[
            pl.BlockSpec((XB_NB, CHUNK, D_MODEL), lambda c, h: (h, c, 0)),
            pl.BlockSpec((1, D_MODEL), lambda c, h: (0, 0)),
            pl.BlockSpec((D_MODEL, B_WIDTH), lambda c, h: (0, 0)),
            pl.BlockSpec((4, GROUPS_PER_TILE * LANES, 2 * LANES), lambda c, h: (0, 0, 0)),
        ],
        out_specs=pl.BlockSpec((B_GROUPS, nrow, S5_FLAT), lambda c, h: (0, c * n_bh + h, 0)),
        out_shape=jax.ShapeDtypeStruct((B_GROUPS, bsz * seq // S5_T, S5_FLAT), BF16),
        scratch_shapes=[pltpu.VMEM((LANE_TILES_B, XB_NB * CHUNK, LANES), F32)],
        compiler_params=pltpu.CompilerParams(
            dimension_semantics=("arbitrary", "arbitrary"), vmem_limit_bytes=V7X_VMEM_LIMIT_BYTES),
        name="s5_in_proj",
    )(x, norm_gain, w_xb, q)


def _s5_kernel(z_ref, m_ref, wst_ref, wout_ref, a_ref, o_ref, s_scr, hp_scr, *, n_c, bsz):
    z0 = z_ref[0, 0]
    z1 = z_ref[0, 1]
    s = _dot(jnp.concatenate([z0, z1], axis=1), wst_ref[0])
    s_scr[0] = s[:, 0:2 * STATE]
    s_scr[1] = s[:, 2 * STATE:4 * STATE]
    a_re = jnp.broadcast_to(a_ref[0, 0:1, :], (bsz, 2 * STATE))
    a_im = jnp.broadcast_to(a_ref[0, 1:2, :], (bsz, 2 * STATE))

    def chunk_step(c, carry):
        h_re, h_im = carry
        for i in range(ROWS_PER_CHUNK):
            rows = pl.ds(c * (bsz * ROWS_PER_CHUNK) + i, bsz, stride=ROWS_PER_CHUNK)
            hp_scr[0, rows, :] = h_re
            hp_scr[1, rows, :] = h_im
            s_re = s_scr[0, rows, :]
            s_im = s_scr[1, rows, :]
            h_re, h_im = (a_re * h_re - a_im * h_im + s_re,
                          a_re * h_im + a_im * h_re + s_im)
        return h_re, h_im

    zero = jnp.zeros((bsz, 2 * STATE), F32)
    lax.fori_loop(0, n_c, chunk_step, (zero, zero))
    hp = jnp.concatenate([hp_scr[0], hp_scr[1]], axis=1).astype(BF16)
    wout = wout_ref[0]
    o_ref[0, 0] = (_dot(z0, m_ref[0, 0]) + _dot(hp, wout[:, 0:S5_FLAT])).astype(BF16)
    o_ref[0, 1] = (_dot(z1, m_ref[0, 1]) + _dot(hp, wout[:, S5_FLAT:2 * S5_FLAT])).astype(BF16)


def _s5_call(zf, m, wst, wout, a16, bsz):
    rows = zf.shape[2]
    n_c = rows // (bsz * ROWS_PER_CHUNK)
    kern = functools.partial(_s5_kernel, n_c=n_c, bsz=bsz)
    return pl.pallas_call(
        kern,
        grid=(S5_PAIRS,),
        in_specs=[
            pl.BlockSpec((1, 2, rows, S5_FLAT), lambda j: (j, 0, 0, 0)),
            pl.BlockSpec((1, 2, S5_FLAT, S5_FLAT), lambda j: (j, 0, 0, 0)),
            pl.BlockSpec((1, 2 * S5_FLAT, 4 * STATE), lambda j: (j, 0, 0)),
            pl.BlockSpec((1, 4 * STATE, 2 * S5_FLAT), lambda j: (j, 0, 0)),
            pl.BlockSpec((1, 2, 2 * STATE), lambda j: (j, 0, 0)),
        ],
        out_specs=pl.BlockSpec((1, 2, rows, S5_FLAT), lambda j: (j, 0, 0, 0)),
        out_shape=jax.ShapeDtypeStruct(zf.shape, BF16),
        scratch_shapes=[
            pltpu.VMEM((2, rows, 2 * STATE), F32),
            pltpu.VMEM((2, rows, 2 * STATE), F32),
        ],
        compiler_params=pltpu.CompilerParams(
            dimension_semantics=("arbitrary",), vmem_limit_bytes=V7X_VMEM_LIMIT_BYTES),
        name="s5_scan",
    )(zf, m, wst, wout, a16)


def _s5_matrices(lambda_re, lambda_im, log_dt, b_re, b_im, c_re, c_im, d_skip):
    g, p = B_GROUPS, STATE
    dt = jnp.exp(log_dt.astype(F32))[:, None]
    lr = lambda_re.astype(F32)
    li = lambda_im.astype(F32)
    mag = jnp.exp(lr * dt)
    ab_re = mag * jnp.cos(li * dt)
    ab_im = mag * jnp.sin(li * dt)
    den = lr * lr + li * li
    nr = ab_re - 1.0
    ni = ab_im
    k_re = ((nr * lr + ni * li) / den)[..., None]
    k_im = ((ni * lr - nr * li) / den)[..., None]
    br = b_re.astype(F32)
    bi = b_im.astype(F32)
    bb_re = k_re * br - k_im * bi
    bb_im = k_re * bi + k_im * br
    lag = jnp.arange(S5_T + 1, dtype=F32)[:, None, None]
    pmag = jnp.exp(lag * (lr * dt)[None])
    pw_re = pmag * jnp.cos(lag * (li * dt)[None])
    pw_im = pmag * jnp.sin(lag * (li * dt)[None])
    cr = c_re.astype(F32)
    ci = c_im.astype(F32)
    e_re = pw_re[..., None] * bb_re[None] - pw_im[..., None] * bb_im[None]
    e_im = pw_re[..., None] * bb_im[None] + pw_im[..., None] * bb_re[None]
    kern = (jnp.einsum('gop,lgpi->lgoi', cr, e_re, precision=_HI)
            - jnp.einsum('gop,lgpi->lgoi', ci, e_im, precision=_HI))
    s_idx = jnp.arange(S5_T)[:, None]
    t_idx = jnp.arange(S5_T)[None, :]
    lagm = jnp.clip(t_idx - s_idx, 0, S5_T)
    kt = kern[lagm]
    kt = jnp.where((t_idx >= s_idx)[:, :, None, None, None], kt, 0.0)
    m = jnp.transpose(kt, (2, 0, 4, 1, 3))
    dsk = d_skip.astype(F32).reshape(g, B_GROUP)
    eye_t = jnp.eye(S5_T, dtype=F32)
    eye_h = jnp.eye(B_GROUP, dtype=F32)
    m = m + (dsk[:, None, :, None, None] * eye_t[None, :, None, :, None]
             * eye_h[None, None, :, None, :])
    m = m.reshape(S5_PAIRS, 2, S5_FLAT, S5_FLAT)
    wst_re = jnp.transpose(e_re[S5_T - 1::-1], (1, 0, 3, 2)).reshape(g, S5_FLAT, p)
    wst_im = jnp.transpose(e_im[S5_T - 1::-1], (1, 0, 3, 2)).reshape(g, S5_FLAT, p)
    f_re = cr[None] * pw_re[:, :, None, :] - ci[None] * pw_im[:, :, None, :]
    f_im = cr[None] * pw_im[:, :, None, :] + ci[None] * pw_re[:, :, None, :]
    wo_re = jnp.transpose(f_re[1:], (1, 3, 0, 2)).reshape(g, p, S5_FLAT)
    wo_im = -jnp.transpose(f_im[1:], (1, 3, 0, 2)).reshape(g, p, S5_FLAT)
    eye2 = jnp.eye(2, dtype=F32)

    def pair_rows(w):
        k, n = w.shape[1], w.shape[2]
        w = w.reshape(S5_PAIRS, 2, k, n)
        return (w[:, :, :, None, :] * eye2[None, :, None, :, None]).reshape(S5_PAIRS, 2 * k, 2 * n)

    wst = jnp.concatenate([pair_rows(wst_re), pair_rows(wst_im)], axis=2)
    wout = jnp.concatenate([pair_rows(wo_re), pair_rows(wo_im)], axis=1)
    a16 = jnp.stack([pw_re[S5_T].reshape(S5_PAIRS, 2 * p),
                     pw_im[S5_T].reshape(S5_PAIRS, 2 * p)], axis=1)
    return m.astype(BF16), wst.astype(BF16), wout.astype(BF16), a16


def _unflat_kernel(y_ref, q_ref, o_ref):
    rcount = y_ref.shape[1]
    ntile = o_ref.shape[1]
    rows_per_tile = rcount // ntile
    blocks = []
    for j in range(LANE_TILES_B):
        for half in range(2):
            blocks.append(jnp.concatenate(
                [y_ref[GROUPS_PER_TILE * j + gl, :, half * LANES:(half + 1) * LANES]
                 for gl in range(GROUPS_PER_TILE)], axis=1))
    lhs = jnp.concatenate(blocks, axis=0)
    for tlp in range(4):
        res = _dot(lhs, q_ref[tlp])
        for j in range(LANE_TILES_B):
            for half in range(2):
                base = (j * 2 + half) * rcount
                for tlo in range(2):
                    t = half * SUBLANES + 2 * tlp + tlo
                    piece = res[base:base + rcount, tlo * LANES:(tlo + 1) * LANES]
                    o_ref[j, :, t * rows_per_tile:(t + 1) * rows_per_tile, :] = (
                        piece.reshape(ntile, rows_per_tile, LANES))


def _unflat_call(yf, q, bsz):
    rows = yf.shape[1]
    rcount = bsz * ROWS_PER_CHUNK
    n_c = rows // rcount
    ntile = bsz // MAIN_NB
    rows_per_tile = MAIN_NB * ROWS_PER_CHUNK
    return pl.pallas_call(
        _unflat_kernel,
        grid=(n_c,),
        in_specs=[
            pl.BlockSpec((B_GROUPS, rcount, S5_FLAT), lambda c: (0, c, 0)),
            pl.BlockSpec((4, GROUPS_PER_TILE * LANES, 2 * LANES), lambda c: (0, 0, 0)),
        ],
        out_specs=pl.BlockSpec((LANE_TILES_B, ntile, S5_T * rows_per_tile, LANES),
                               lambda c: (0, c, 0, 0)),
        out_shape=jax.ShapeDtypeStruct(
            (LANE_TILES_B, n_c * ntile, S5_T * rows_per_tile, LANES), F32),
        compiler_params=pltpu.CompilerParams(
            dimension_semantics=("arbitrary",), vmem_limit_bytes=V7X_VMEM_LIMIT_BYTES),
        name="s5_unflat",
    )(yf, q)


def _main_kernel(x_ref, ys_ref, ng_ref, w1_ref, lng_ref, lnb_ref, ws_ref, sb_ref, wad_ref,
                 wglu_ref, wbd_ref, wout_ref, fg_ref, o_ref, vn_s, mix_s):
    nb = x_ref.shape[0]
    tm = nb * CHUNK
    x = x_ref[...].reshape(tm, D_MODEL)
    xn = _rmsnorm(x, ng_ref[...]).astype(BF16)

    def proj(c0, c1):
        return _dot(xn, w1_ref[:, c0:c1])

    v = _gelu(proj(C_V, C_V + A_WIDTH))
    mu = jnp.mean(v, axis=-1, keepdims=True)
    vc = v - mu
    var = jnp.mean(vc * vc, axis=-1, keepdims=True)
    vn_s[...] = ((vc * lax.rsqrt(var + EPS)) * lng_ref[...] + lnb_ref[...]).astype(BF16)
    for h in range(A_HEADS):
        cols = slice(h * A_HEAD_DIM, (h + 1) * A_HEAD_DIM)
        vh = jnp.concatenate(
            [vn_s[c * CHUNK:(c + 1) * CHUNK, cols] for c in range(nb)], axis=1)
        mh = _dot(ws_ref[h], vh)
        for c in range(nb):
            mix_s[c * CHUNK:(c + 1) * CHUNK, cols] = (
                mh[:, c * A_HEAD_DIM:(c + 1) * A_HEAD_DIM] + sb_ref[:, cols])
    u = _gelu(proj(C_U, C_U + A_WIDTH))
    za = _silu(proj(C_Z, C_Z + A_WIDTH))
    ya = _dot((u * mix_s[...] * za).astype(BF16), wad_ref[...])
    merged = _sigmoid(proj(C_GA, C_GA + D_MODEL)) * ya
    ys = jnp.concatenate(
        [jnp.concatenate(
            [ys_ref[j, 0, pl.ds(k, S5_T, stride=nb * ROWS_PER_CHUNK), :]
             for k in range(nb * ROWS_PER_CHUNK)], axis=0)
         for j in range(LANE_TILES_B)], axis=1)
    t = _gelu(ys).astype(BF16)
    pq = _dot(t, wglu_ref[...])
    zb = _silu(proj(C_ZB, C_ZB + B_WIDTH))
    yb_pre = pq[:, 0:B_WIDTH] * _sigmoid(pq[:, B_WIDTH:2 * B_WIDTH]) * zb
    yb = _dot(yb_pre.astype(BF16), wbd_ref[...])
    merged = merged + _sigmoid(proj(C_GB, C_GB + D_MODEL)) * yb
    hres = x + _dot(merged.astype(BF16), wout_ref[...])
    o_ref[...] = _rmsnorm(hres, fg_ref[...]).reshape(nb, CHUNK, D_MODEL)


def _main_call(x, ysp, norm_gain, w1, ln_g, ln_b, ws, sb, wad, wglu, wbd, wout, fgain):
    bsz, seq, _ = x.shape
    n_c, n_bp = seq // CHUNK, bsz // MAIN_NB
    tm = MAIN_NB * CHUNK

    def const(shape):
        nd = len(shape)
        return pl.BlockSpec(shape, lambda c, b: (0,) * nd, pipeline_mode=pl.Buffered(1))

    return pl.pallas_call(
        _main_kernel,
        grid=(n_c, n_bp),
        in_specs=[
            pl.BlockSpec((MAIN_NB, CHUNK, D_MODEL), lambda c, b: (b, c, 0)),
            pl.BlockSpec((LANE_TILES_B, 1, S5_T * MAIN_NB * ROWS_PER_CHUNK, LANES),
                         lambda c, b: (0, c * n_bp + b, 0, 0)),
            const((1, D_MODEL)),
            const((D_MODEL, W1_COLS)),
            const((1, A_WIDTH)),
            const((1, A_WIDTH)),
            const((A_HEADS, CHUNK, CHUNK)),
            const((CHUNK, A_WIDTH)),
            const((A_WIDTH, D_MODEL)),
            const((B_WIDTH, 2 * B_WIDTH)),
            const((B_WIDTH, D_MODEL)),
            const((D_MODEL, D_MODEL)),
            const((1, D_MODEL)),
        ],
        out_specs=pl.BlockSpec((MAIN_NB, CHUNK, D_MODEL), lambda c, b: (b, c, 0)),
        out_shape=jax.ShapeDtypeStruct((bsz, seq, D_MODEL), F32),
        scratch_shapes=[
            pltpu.VMEM((tm, A_WIDTH), BF16),
            pltpu.VMEM((tm, A_WIDTH), F32),
        ],
        compiler_params=pltpu.CompilerParams(
            dimension_semantics=("arbitrary", "arbitrary"), vmem_limit_bytes=V7X_VMEM_LIMIT_BYTES),
        name="fused_block",
    )(x, ysp, norm_gain, w1, ln_g, ln_b, ws, sb, wad, wglu, wbd, wout, fgain)


def kernel(x, norm_gain, w_in, a_ln_gain, a_ln_bias, a_spatial, a_spatial_bias, w_a_down,
           lambda_re, lambda_im, log_dt, b_re, b_im, c_re, c_im, d_skip, w_glu, w_b_down,
           w_out, final_gain):
    assert norm_gain.shape[0] == 1, "single-layer block"
    bsz, seq, d = x.shape
    assert d == D_MODEL and seq % CHUNK == 0 and bsz % XB_NB == 0
    w_in0 = w_in[0]
    ngain = norm_gain[0].reshape(1, D_MODEL).astype(F32)
    q = _placement_matrices()

    w_xb = w_in0[:, 3 * A_WIDTH:3 * A_WIDTH + B_WIDTH].astype(BF16)
    zf = _xb_call(x, ngain, w_xb, q)
    rows = zf.shape[1]

    m, wst, wo, a16 = _s5_matrices(lambda_re[0], lambda_im[0], log_dt[0], b_re[0], b_im[0],
                                   c_re[0], c_im[0], d_skip[0])
    yf = _s5_call(zf.reshape(S5_PAIRS, 2, rows, S5_FLAT), m, wst, wo, a16, bsz)

    ysp = _unflat_call(yf.reshape(B_GROUPS, rows, S5_FLAT), q, bsz)

    w1 = jnp.concatenate(
        [w_in0[:, 0:3 * A_WIDTH], w_in0[:, 3 * A_WIDTH + B_WIDTH:]], axis=1).astype(BF16)
    causal = jnp.tril(jnp.ones((CHUNK, CHUNK), dtype=bool))
    ws = jnp.where(causal[None], a_spatial[0], 0).astype(BF16)
    sb = jnp.repeat(jnp.transpose(a_spatial_bias[0]).astype(F32), A_HEAD_DIM, axis=1)
    return _main_call(
        x, ysp, ngain, w1,
        a_ln_gain[0].reshape(1, A_WIDTH).astype(F32), a_ln_bias[0].reshape(1, A_WIDTH).astype(F32),
        ws, sb, w_a_down[0].astype(BF16), w_glu[0].astype(BF16), w_b_down[0].astype(BF16),
        w_out[0].astype(BF16), final_gain.reshape(1, D_MODEL).astype(F32))
```
